```python
import jax, jax.numpy as jnp
from jax import lax
import numpy as np

D_MODEL = 2048
BATCH = 4
SEQ = 2048
DEPTH = 1
DEC_BATCH = 128
DEC_SEQ = 4
PAST_LEN = 8192
PAGE_SIZE = 128

ROPE_THETA = 500000.0
NORM_EPS = 1e-6
D_FF = 5632
N_ADA = 9
MLA_HEADS = 8
MLA_NOPE = 128
MLA_ROPE = 64
MLA_V = 128
MLA_LORA = 512
MLA_ROW = MLA_LORA + MLA_ROPE
MLA_SCALE = (MLA_NOPE + MLA_ROPE) ** -0.5
NSA_HEADS = 8
NSA_KV = 2
NSA_GROUP = NSA_HEADS // NSA_KV
NSA_HD = 128
NSA_ROT = NSA_HD // 4
NSA_SCALE = NSA_HD ** -0.5
CMP_BLOCK = 32
SEL_BLOCK = 64
N_SELECT = 16
WINDOW = 512
Q_BLOCK = 128
NEG_INF = -1e30
FORCE_SCORE = 1e4
W_QA = MLA_HEADS * (MLA_NOPE + MLA_ROPE)
W_KVA = MLA_ROW
W_QB = NSA_HEADS * NSA_HD
W_KVB = 6 * NSA_KV * NSA_HD
W_GB = 3 * NSA_HEADS
W_MERGE = 2 * D_MODEL
IN_WIDTH = W_QA + W_KVA + W_QB + W_KVB + W_GB + W_MERGE
SPLITS = (W_QA, W_QA + W_KVA, W_QA + W_KVA + W_QB, W_QA + W_KVA + W_QB + W_KVB,
          W_QA + W_KVA + W_QB + W_KVB + W_GB)

kernel_name = "hybrid_mla_nsa_macaron_decode_step"


def rms_norm(x, g):
    xf = x.astype(jnp.float32)
    y = xf * lax.rsqrt(jnp.mean(xf * xf, axis=-1, keepdims=True) + NORM_EPS)
    return (y * g.astype(jnp.float32)).astype(x.dtype)


def modulate(x, g, shift, scale):
    return rms_norm(x, g) * (1.0 + scale[:, None, :]) + shift[:, None, :]


def rope(x, pos, n_rot):
    half = n_rot // 2
    inv = ROPE_THETA ** (-jnp.arange(half, dtype=jnp.float32) / half)
    ang = pos.astype(jnp.float32)[:, None] * inv[None, :]
    shape = (ang.shape[0],) + (1,) * (x.ndim - 3) + (half,)
    cos, sin = jnp.cos(ang).reshape(shape), jnp.sin(ang).reshape(shape)
    xr = x[..., :n_rot].astype(jnp.float32)
    x1, x2 = xr[..., :half], xr[..., half:]
    rot = jnp.concatenate([x1 * cos - x2 * sin, x2 * cos + x1 * sin], axis=-1).astype(x.dtype)
    return jnp.concatenate([rot, x[..., n_rot:]], axis=-1)


def masked_softmax(s, mask):
    s = jnp.where(mask, s.astype(jnp.float32), NEG_INF)
    p = jax.nn.softmax(s, axis=-1)
    return jnp.where(mask, p, 0.0)


def swiglu(h, wg, wu, wd):
    return (jax.nn.silu(h @ wg) * (h @ wu)) @ wd


def adaln_params(c, w_ada, b_ada):
    return (jax.nn.silu(c) @ w_ada + b_ada).reshape(c.shape[0], N_ADA, D_MODEL)


def ffn_sublayer(x, ada3, g_pre, g_post, wg, wu, wd):
    h = modulate(x, g_pre, ada3[:, 0], ada3[:, 1])
    return x + 0.5 * ada3[:, 2][:, None, :] * rms_norm(swiglu(h, wg, wu, wd), g_post)


def to_blocks(x):
    n, s = x.shape[:2]
    return jnp.swapaxes(x.reshape((n, s // Q_BLOCK, Q_BLOCK) + x.shape[2:]), 0, 1)


def from_blocks(x):
    nb, n, qb = x.shape[:3]
    return jnp.swapaxes(x, 0, 1).reshape((n, nb * qb) + x.shape[3:])


def project_heads(z, pos, g_lat, w_uk):
    zq_a, zkv_a, zq_b, zkv_b, zg_b, zm = jnp.split(z, SPLITS, axis=-1)
    n, t = z.shape[:2]
    qa = zq_a.reshape(n, t, MLA_HEADS, MLA_NOPE + MLA_ROPE)
    q_lat = jnp.einsum('nthd,chd->nthc', qa[..., :MLA_NOPE], w_uk)
    q_mla = jnp.concatenate([q_lat, rope(qa[..., MLA_NOPE:], pos, MLA_ROPE)], axis=-1)
    mla_row = jnp.concatenate([rms_norm(zkv_a[..., :MLA_LORA], g_lat),
                               rope(zkv_a[..., MLA_LORA:], pos, MLA_ROPE)], axis=-1)
    qb = rope(zq_b.reshape(n, t, NSA_HEADS, NSA_HD), pos, NSA_ROT)
    qb = qb.reshape(n, t, NSA_KV, NSA_GROUP, NSA_HD)
    kv = zkv_b.reshape(n, t, 3, 2, NSA_KV, NSA_HD)
    kv = jnp.stack([rope(kv[:, :, :, 0], pos, NSA_ROT), kv[:, :, :, 1]], axis=3)
    kv = kv.reshape(n, t, 6, NSA_KV, NSA_HD)
    gates = jax.nn.sigmoid(zg_b.astype(jnp.float32)).astype(z.dtype)
    gates = gates.reshape(n, t, 3, NSA_KV, NSA_GROUP)
    return q_mla, mla_row, qb, kv, gates, zm


def mla_core(q_cat, rows_list, mask_list):
    s = jnp.concatenate([jnp.einsum('nthc,nkc->nhtk', q_cat, r) for r in rows_list], axis=-1) * MLA_SCALE
    p = masked_softmax(s, jnp.concatenate(mask_list, axis=-1)).astype(q_cat.dtype)
    o, start = None, 0
    for r in rows_list:
        n_k = r.shape[1]
        term = jnp.einsum('nhtk,nkc->nthc', p[..., start:start + n_k], r[..., :MLA_LORA])
        o = term if o is None else o + term
        start += n_k
    return o


def mla_up(o_lat, w_uv):
    n, t = o_lat.shape[:2]
    return jnp.einsum('nthc,chd->nthd', o_lat, w_uv).reshape(n, t, MLA_HEADS * MLA_V)


def compress(kv2, w_pool):
    n, l = kv2.shape[:2]
    nb = l // CMP_BLOCK
    kb = kv2[:, :nb * CMP_BLOCK].reshape((n, nb, CMP_BLOCK) + kv2.shape[2:])
    return jnp.einsum('nbisgd,si->nbsgd', kb, w_pool)


def nsa_cmp_attend(qg, kc, vc, qpos):
    nb = kc.shape[1]
    s = jnp.einsum('ntgpd,nbgd->ntgpb', qg, kc) * NSA_SCALE
    visible = (jnp.arange(nb)[None, :] + 1) * CMP_BLOCK - 1 <= qpos[:, None]
    p = masked_softmax(s, visible[None, :, None, None, :])
    o = jnp.einsum('ntgpb,nbgd->ntgpd', p.astype(vc.dtype), vc)
    return o, p


def nsa_select(p_cmp, qpos, n_sel):
    imp = p_cmp.sum(axis=3)
    ratio = SEL_BLOCK // CMP_BLOCK
    imp = jnp.pad(imp, ((0, 0), (0, 0), (0, 0), (0, n_sel * ratio - imp.shape[-1])))
    imp = imp.reshape(imp.shape[:-1] + (n_sel, ratio)).sum(-1)
    blk = jnp.arange(n_sel)[None, :]
    cur = (qpos // SEL_BLOCK)[:, None]
    valid = blk <= cur
    forced = (blk == 0) | (blk == cur) | (blk == cur - 1)
    score = jnp.where(valid[None, :, None], jnp.where(forced[None, :, None], FORCE_SCORE, imp), NEG_INF)
    top, idx = lax.top_k(score, min(N_SELECT, n_sel))
    return idx, top > 0.5 * NEG_INF


def nsa_sel_attend(qg, ks, vs, kpos, ok, qpos):
    n, t, g, k, b, hd = ks.shape
    mask = ok[..., None] & (kpos <= qpos[None, :, None, None, None])
    s = jnp.einsum('ntgpd,ntgkbd->ntgpkb', qg, ks) * NSA_SCALE
    p = masked_softmax(s.reshape(n, t, g, NSA_GROUP, k * b), mask.reshape(n, t, g, 1, k * b))
    return jnp.einsum('ntgpj,ntgjd->ntgpd', p.astype(vs.dtype), vs.reshape(n, t, g, k * b, hd))


def gqa_attend(qg, k, v, mask):
    s = jnp.einsum('...tgpd,...kgd->...gptk', qg, k) * NSA_SCALE
    p = masked_softmax(s, mask).astype(v.dtype)
    return jnp.einsum('...gptk,...kgd->...tgpd', p, v)


def nsa_window_prompt(qb, wkv):
    n, s = qb.shape[:2]
    nb = s // Q_BLOCK
    span = Q_BLOCK + WINDOW
    kidx = jnp.arange(nb)[:, None] * Q_BLOCK + jnp.arange(span)[None, :]
    kw = jnp.pad(wkv, ((0, 0), (WINDOW, 0), (0, 0), (0, 0), (0, 0)))[:, kidx]
    kpos = kidx - WINDOW
    qpos = jnp.arange(s).reshape(nb, Q_BLOCK)
    d = qpos[:, :, None] - kpos[:, None, :]
    mask = (d >= 0) & (d < WINDOW) & (kpos[:, None, :] >= 0)
    qblk = qb.reshape(n, nb, Q_BLOCK, NSA_KV, NSA_GROUP, NSA_HD)
    o = gqa_attend(qblk, kw[:, :, :, 0], kw[:, :, :, 1], mask[:, None, None])
    return o.reshape(n, s, NSA_KV, NSA_GROUP, NSA_HD)


def nsa_combine(gates, o_cmp, o_sel, o_win):
    o = gates[:, :, 0, ..., None] * o_cmp + gates[:, :, 1, ..., None] * o_sel + gates[:, :, 2, ..., None] * o_win
    return o.reshape(o.shape[0], o.shape[1], NSA_HEADS * NSA_HD)


def merge_branches(o_mla, o_nsa, zm, p_a, p_b, w_out):
    gate = jax.nn.sigmoid(zm.astype(jnp.float32)).astype(zm.dtype)
    u = gate[..., :D_MODEL] * (o_mla @ p_a) + gate[..., D_MODEL:] * (o_nsa @ p_b)
    return u @ w_out


def mixer_prompt(z, g_lat, w_uk, w_uv, w_pool, p_a, p_b, w_out):
    n, s = z.shape[:2]
    pos = jnp.arange(s)
    q_mla, mla_row, qb, kv, gates, zm = project_heads(z, pos, g_lat, w_uk)
    qpos_blocks = pos.reshape(s // Q_BLOCK, Q_BLOCK)
    def mla_block(args):
        qpos, q = args
        return mla_core(q, [mla_row], [pos[None, :] <= qpos[:, None]])
    o_mla = mla_up(from_blocks(lax.map(mla_block, (qpos_blocks, to_blocks(q_mla)))), w_uv)
    pooled = compress(kv[:, :, :2], w_pool)
    o_cmp, p_cmp = nsa_cmp_attend(qb, pooled[:, :, 0], pooled[:, :, 1], pos)
    idx, ok = nsa_select(p_cmp, pos, s // SEL_BLOCK)
    k_sel, v_sel = kv[:, :, 2], kv[:, :, 3]
    n_i = jnp.arange(n)[:, None, None, None, None]
    g_i = jnp.arange(NSA_KV)[None, None, :, None, None]
    def sel_block(args):
        qpos, q, ib, okb = args
        kpos = ib[..., None] * SEL_BLOCK + jnp.arange(SEL_BLOCK)
        return nsa_sel_attend(q, k_sel[n_i, kpos, g_i], v_sel[n_i, kpos, g_i], kpos, okb, qpos)
    o_sel = from_blocks(lax.map(sel_block, (qpos_blocks, to_blocks(qb), to_blocks(idx), to_blocks(ok))))
    o_win = nsa_window_prompt(qb, kv[:, :, 4:6])
    out = merge_branches(o_mla, nsa_combine(gates, o_cmp, o_sel, o_win), zm, p_a, p_b, w_out)
    n_keep = min(WINDOW, s)
    return out, (mla_row, kv[:, :, :4], kv[:, s - n_keep:, 4:6])


def mixer_sample(z, cache_mla, cache_nsa_kv, state_win_kv, page_table, layer,
                 g_lat, w_uk, w_uv, w_pool, p_a, p_b, w_out):
    n, t = z.shape[:2]
    n_pages = page_table.shape[1]
    page = cache_mla.shape[2]
    past = n_pages * page
    pos = past + jnp.arange(t)
    q_mla, mla_row, qb, kv, gates, zm = project_heads(z, pos, g_lat, w_uk)
    past_rows = cache_mla[layer, page_table].reshape(n, past, MLA_ROW)
    kpos_past = jnp.arange(past)
    o_lat = mla_core(q_mla, [past_rows, mla_row],
                     [kpos_past[None, :] <= pos[:, None], pos[None, :] <= pos[:, None]])
    o_mla = mla_up(o_lat, w_uv)
    past_cmp = cache_nsa_kv[layer, page_table, :, :2].reshape(n, past, 2, NSA_KV, NSA_HD)
    pooled = jnp.concatenate([compress(past_cmp, w_pool), compress(kv[:, :, :2], w_pool)], axis=1)
    o_cmp, p_cmp = nsa_cmp_attend(qb, pooled[:, :, 0], pooled[:, :, 1], pos)
    n_sel = -(-(past + t) // SEL_BLOCK)
    idx, ok = nsa_select(p_cmp, pos, n_sel)
    start = idx * SEL_BLOCK
    n_i = jnp.arange(n)[:, None, None, None]
    g_i = jnp.arange(NSA_KV)[None, None, :, None]
    slot = jnp.arange(2)[:, None, None, None, None]
    pool_blocks = cache_nsa_kv.reshape(cache_nsa_kv.shape[:2] + (page // SEL_BLOCK, SEL_BLOCK) + cache_nsa_kv.shape[3:])
    phys = page_table[n_i, jnp.minimum(start // page, n_pages - 1)]
    sub = (start % page) // SEL_BLOCK
    past_sel = pool_blocks[layer, phys, sub, :, slot + 2, g_i]
    n_new = -(-t // SEL_BLOCK)
    new_sel = jnp.pad(kv[:, :, 2:4], ((0, 0), (0, n_new * SEL_BLOCK - t), (0, 0), (0, 0), (0, 0)))
    new_sel = new_sel.reshape(n, n_new, SEL_BLOCK, 2, NSA_KV, NSA_HD)
    new_blk = jnp.clip((start - past) // SEL_BLOCK, 0, n_new - 1)
    new_g = new_sel[n_i, new_blk, :, slot, g_i]
    sel = jnp.where((start >= past)[None, ..., None, None], new_g, past_sel)
    kpos = start[..., None] + jnp.arange(SEL_BLOCK)
    o_sel = nsa_sel_attend(qb, sel[0], sel[1], kpos, ok, pos)
    win_rows = jnp.concatenate([state_win_kv[layer], kv[:, :, 4:6]], axis=1)
    wb = state_win_kv.shape[2]
    kpos_w = past - wb + jnp.arange(wb + t)
    d = pos[:, None] - kpos_w[None, :]
    o_win = gqa_attend(qb, win_rows[:, :, 0], win_rows[:, :, 1], (d >= 0) & (d < WINDOW))
    out = merge_branches(o_mla, nsa_combine(gates, o_cmp, o_sel, o_win), zm, p_a, p_b, w_out)
    return out, (mla_row, kv[:, :, :4], win_rows[:, t:])


def setup_inputs(seed: int = 0) -> dict:
    key = jax.random.key(seed)
    ks = list(jax.random.split(key, 32))
    f32 = jnp.float32
    def nrm(shape, scale=1.0):
        return scale * jax.random.normal(ks.pop(), shape, f32)
    def gain(shape):
        return 1.0 + 0.05 * jax.random.normal(ks.pop(), shape, f32)
    n_pages = PAST_LEN // PAGE_SIZE
    n_used = DEC_BATCH * n_pages
    n_pool = n_used + (n_used + 3) // 4
    win_buf = min(WINDOW, PAST_LEN)
    L = DEPTH
    d_s = D_MODEL ** -0.5
    page_table = jax.random.permutation(ks.pop(), n_pool)[:n_used].reshape(DEC_BATCH, n_pages).astype(jnp.int32)
    return {
        "x_prompt": nrm((BATCH, SEQ, D_MODEL)),
        "x_sample": nrm((DEC_BATCH, DEC_SEQ, D_MODEL)),
        "c_prompt": nrm((BATCH, D_MODEL)),
        "c_sample": nrm((DEC_BATCH, D_MODEL)),
        "cache_mla": nrm((L, n_pool, PAGE_SIZE, MLA_ROW)),
        "cache_nsa_kv": nrm((L, n_pool, PAGE_SIZE, 4, NSA_KV, NSA_HD)),
        "state_win_kv": nrm((L, DEC_BATCH, win_buf, 2, NSA_KV, NSA_HD)),
        "page_table": page_table,
        "w_ada": nrm((L, D_MODEL, N_ADA * D_MODEL), d_s),
        "b_ada": nrm((L, N_ADA * D_MODEL), 0.02),
        "g_pre_ffn1": gain((L, D_MODEL)),
        "g_post_ffn1": gain((L, D_MODEL)),
        "w1_gate": nrm((L, D_MODEL, D_FF), d_s),
        "w1_up": nrm((L, D_MODEL, D_FF), d_s),
        "w1_down": nrm((L, D_FF, D_MODEL), D_FF ** -0.5),
        "g_pre_mix": gain((L, D_MODEL)),
        "g_post_mix": gain((L, D_MODEL)),
        "w_in": nrm((L, D_MODEL, IN_WIDTH), d_s),
        "g_lat": gain((L, MLA_LORA)),
        "w_uk": nrm((L, MLA_LORA, MLA_HEADS, MLA_NOPE), MLA_LORA ** -0.5),
        "w_uv": nrm((L, MLA_LORA, MLA_HEADS, MLA_V), MLA_LORA ** -0.5),
        "w_pool": (1.0 + nrm((L, 2, CMP_BLOCK), 0.1)) / CMP_BLOCK,
        "p_a": nrm((L, MLA_HEADS * MLA_V, D_MODEL), (MLA_HEADS * MLA_V) ** -0.5),
        "p_b": nrm((L, NSA_HEADS * NSA_HD, D_MODEL), (NSA_HEADS * NSA_HD) ** -0.5),
        "w_out": nrm((L, D_MODEL, D_MODEL), d_s),
        "g_pre_ffn2": gain((L, D_MODEL)),
        "g_post_ffn2": gain((L, D_MODEL)),
        "w2_gate": nrm((L, D_MODEL, D_FF), d_s),
        "w2_up": nrm((L, D_MODEL, D_FF), d_s),
        "w2_down": nrm((L, D_FF, D_MODEL), D_FF ** -0.5),
    }


def reference(x_prompt, x_sample, c_prompt, c_sample, cache_mla, cache_nsa_kv, state_win_kv, page_table,
              w_ada, b_ada, g_pre_ffn1, g_post_ffn1, w1_gate, w1_up, w1_down,
              g_pre_mix, g_post_mix, w_in, g_lat, w_uk, w_uv, w_pool, p_a, p_b, w_out,
              g_pre_ffn2, g_post_ffn2, w2_gate, w2_up, w2_down):
    xp, xs = x_prompt, x_sample
    st = [[] for _ in range(6)]
    for l in range(DEPTH):
        ada_p = adaln_params(c_prompt, w_ada[l], b_ada[l])
        ada_s = adaln_params(c_sample, w_ada[l], b_ada[l])
        xp = ffn_sublayer(xp, ada_p[:, 0:3], g_pre_ffn1[l], g_post_ffn1[l], w1_gate[l], w1_up[l], w1_down[l])
        xs = ffn_sublayer(xs, ada_s[:, 0:3], g_pre_ffn1[l], g_post_ffn1[l], w1_gate[l], w1_up[l], w1_down[l])
        hp = modulate(xp, g_pre_mix[l], ada_p[:, 3], ada_p[:, 4])
        hs = modulate(xs, g_pre_mix[l], ada_s[:, 3], ada_s[:, 4])
        mp, sp = mixer_prompt(hp @ w_in[l], g_lat[l], w_uk[l], w_uv[l], w_pool[l], p_a[l], p_b[l], w_out[l])
        ms, ss = mixer_sample(hs @ w_in[l], cache_mla, cache_nsa_kv, state_win_kv, page_table, l,
                              g_lat[l], w_uk[l], w_uv[l], w_pool[l], p_a[l], p_b[l], w_out[l])
        xp = xp + ada_p[:, 5][:, None, :] * rms_norm(mp, g_post_mix[l])
        xs = xs + ada_s[:, 5][:, None, :] * rms_norm(ms, g_post_mix[l])
        xp = ffn_sublayer(xp, ada_p[:, 6:9], g_pre_ffn2[l], g_post_ffn2[l], w2_gate[l], w2_up[l], w2_down[l])
        xs = ffn_sublayer(xs, ada_s[:, 6:9], g_pre_ffn2[l], g_post_ffn2[l], w2_gate[l], w2_up[l], w2_down[l])
        for lst, arr in zip(st, sp + ss):
            lst.append(arr)
    new_mla_p = jnp.stack(st[0])
    new_nsa_p = jnp.stack(st[1])
    new_win_p = jnp.stack(st[2])
    new_mla_s = jnp.stack(st[3])
    new_nsa_s = jnp.stack(st[4])
    new_win_s = jnp.stack(st[5])
    return (xp, xs, new_mla_p, new_nsa_p, new_win_p, new_mla_s, new_nsa_s, new_win_s)
```

```python
import functools

import jax
import jax.numpy as jnp
from jax import lax
from jax.experimental import pallas as pl
from jax.experimental.pallas import tpu as pltpu

F32 = jnp.float32
BF16 = jnp.bfloat16

D_MODEL = 2048
D_FF = 5632
N_ADA = 9
ROPE_THETA = 500000.0
NORM_EPS = 1e-6
MLA_HEADS = 8
MLA_NOPE = 128
MLA_ROPE = 64
MLA_V = 128
MLA_LORA = 512
MLA_ROW = MLA_LORA + MLA_ROPE
MLA_QCAT = MLA_LORA + 128
MLA_SCALE = (MLA_NOPE + MLA_ROPE) ** -0.5
NSA_HEADS = 8
NSA_KV = 2
NSA_GROUP = NSA_HEADS // NSA_KV
NSA_HD = 128
NSA_ROT = NSA_HD // 4
NSA_SCALE = NSA_HD ** -0.5
CMP_BLOCK = 32
SEL_BLOCK = 64
N_SELECT = 16
WINDOW = 512
Q_BLOCK = 128
NEG_INF = -1e30
FORCE_SCORE = 1e4
LANES = 128
W_QA = MLA_HEADS * (MLA_NOPE + MLA_ROPE)
W_KVA = MLA_ROW
W_QB = NSA_HEADS * NSA_HD
W_KVB = 6 * NSA_KV * NSA_HD
W_GB = 3 * NSA_HEADS
W_MERGE = 2 * D_MODEL
W1_COLS = 2 * MLA_HEADS * LANES + MLA_LORA + LANES
W2_COLS = W_QB + W_KVB + NSA_KV * LANES
VMEM_LIMIT = 56 * 1024 * 1024
PAGES_PER_STEP = 8
PAGE = 128

_NT = (((1,), (1,)), ((), ()))


def _cparams(*sem):
    return pltpu.CompilerParams(dimension_semantics=sem, vmem_limit_bytes=VMEM_LIMIT)


def _rms(x, g):
    return x * lax.rsqrt(jnp.mean(x * x, axis=-1, keepdims=True) + NORM_EPS) * g


def _modulated(x, g, shift, scale):
    return _rms(x, g) * (1.0 + scale) + shift


def _dot(a, b):
    return jnp.dot(a, b, preferred_element_type=F32)


def _dot_nt(a, b):
    return lax.dot_general(a, b, _NT, preferred_element_type=F32)


def _rope(v, c, sa, sb, half):
    return v * c + pltpu.roll(v, LANES - half, 1) * sa + pltpu.roll(v, half, 1) * sb


def _rope_tables(pos, n_rot):
    half = n_rot // 2
    inv = ROPE_THETA ** (-jnp.arange(half, dtype=F32) / half)
    ang = pos.astype(F32)[:, None] * inv[None, :]
    cos, sin = jnp.cos(ang), jnp.sin(ang)
    t = pos.shape[0]
    c = jnp.concatenate([cos, cos, jnp.ones((t, LANES - n_rot), F32)], axis=1)
    sa = jnp.concatenate([-sin, jnp.zeros((t, LANES - half), F32)], axis=1)
    sb = jnp.concatenate([jnp.zeros((t, half), F32), sin, jnp.zeros((t, LANES - n_rot), F32)], axis=1)
    return c, sa, sb


def _flash_update(m_ref, l_ref, acc_ref, s, mask, v_list, row_slices):
    m_old = m_ref[...]
    if mask is not None:
        s = jnp.where(mask, s, NEG_INF)
    m_new = jnp.maximum(m_old, jnp.max(s, axis=-1, keepdims=True))
    alpha = jnp.exp(m_old - m_new)
    p = jnp.exp(s - m_new)
    if mask is not None:
        p = jnp.where(mask, p, 0.0)
    l_ref[...] = alpha * l_ref[...] + jnp.sum(p, axis=-1, keepdims=True)
    m_ref[...] = m_new
    pb = p.astype(BF16)
    for v, rs in zip(v_list, row_slices):
        acc_ref[rs, :] = alpha[rs] * acc_ref[rs, :] + _dot(pb[rs], v)


def _ada_kernel(c_ref, w_ref, b_ref, o_ref):
    c = c_ref[...]
    a = (c * jax.nn.sigmoid(c)).astype(BF16)
    o_ref[...] = _dot(a, w_ref[...].astype(BF16)) + b_ref[...]


def _adaln(c_all, w_ada, b_ada):
    rows = c_all.shape[0]
    tn = 1024
    return pl.pallas_call(
        _ada_kernel,
        grid=(w_ada.shape[1] // tn,),
        in_specs=[pl.BlockSpec((rows, D_MODEL), lambda j: (0, 0)),
                  pl.BlockSpec((D_MODEL, tn), lambda j: (0, j)),
                  pl.BlockSpec((1, tn), lambda j: (0, j))],
        out_specs=pl.BlockSpec((rows, tn), lambda j: (0, j)),
        out_shape=jax.ShapeDtypeStruct((rows, w_ada.shape[1]), F32),
        compiler_params=_cparams("arbitrary"),
        name="adaln",
    )(c_all, w_ada, b_ada.reshape(1, -1))


class _Mods:
    def __init__(self, arr, per_token, tiles_per_seq=1):
        self.arr, self.per_token, self.tiles_per_seq = arr, per_token, tiles_per_seq

    def spec(self, k, tm):
        if self.per_token:
            return pl.BlockSpec((tm, D_MODEL), lambda i, *_: (i, k))
        tps = self.tiles_per_seq
        return pl.BlockSpec((None, None, 1, D_MODEL), lambda i, *_: (i // tps, k, 0, 0))


def _row_spec(width):
    return pl.BlockSpec((1, width), lambda i, *_: (0, 0))


def _ffn_kernel(x_ref, gpre_ref, sh_ref, sc_ref, gate_ref, gpost_ref, wg_ref, wu_ref, wd_ref, o_ref,
                h_ref, acc_ref):
    j = pl.program_id(1)

    @pl.when(j == 0)
    def _():
        h_ref[...] = _modulated(x_ref[...], gpre_ref[...], sh_ref[...], sc_ref[...]).astype(BF16)
        acc_ref[...] = jnp.zeros_like(acc_ref)

    h = h_ref[...]
    g = _dot(h, wg_ref[...])
    u = _dot(h, wu_ref[...])
    a = (g * jax.nn.sigmoid(g) * u).astype(BF16)
    acc_ref[...] += _dot(a, wd_ref[...])

    @pl.when(j == pl.num_programs(1) - 1)
    def _():
        o_ref[...] = x_ref[...] + 0.5 * gate_ref[...] * _rms(acc_ref[...], gpost_ref[...])


def _ffn(x, mods, k0, gpre, gpost, wg, wu, wd, tm, tf=512):
    t = x.shape[0]
    return pl.pallas_call(
        _ffn_kernel,
        grid=(t // tm, D_FF // tf),
        in_specs=[pl.BlockSpec((tm, D_MODEL), lambda i, j: (i, 0)),
                  _row_spec(D_MODEL), mods.spec(k0, tm), mods.spec(k0 + 1, tm), mods.spec(k0 + 2, tm),
                  _row_spec(D_MODEL),
                  pl.BlockSpec((D_MODEL, tf), lambda i, j: (0, j)),
                  pl.BlockSpec((D_MODEL, tf), lambda i, j: (0, j)),
                  pl.BlockSpec((tf, D_MODEL), lambda i, j: (j, 0))],
        out_specs=pl.BlockSpec((tm, D_MODEL), lambda i, j: (i, 0)),
        out_shape=jax.ShapeDtypeStruct((t, D_MODEL), F32),
        scratch_shapes=[pltpu.VMEM((tm, D_MODEL), BF16), pltpu.VMEM((tm, D_MODEL), F32)],
        compiler_params=_cparams("parallel", "arbitrary"),
        name="ffn",
    )(x, gpre, mods.arr, mods.arr, mods.arr, gpost, wg, wu, wd)


def _mla_proj_kernel(x_ref, g_ref, sh_ref, sc_ref, w_ref, glat_ref, wuk_ref, c_ref, sa_ref, sb_ref,
                     q_ref, row_ref, k_ref):
    h = _modulated(x_ref[...], g_ref[...], sh_ref[...], sc_ref[...]).astype(BF16)
    z = _dot(h, w_ref[...])
    c, sa, sb = c_ref[...], sa_ref[...], sb_ref[...]
    half = MLA_ROPE // 2
    o_qr = MLA_HEADS * LANES
    o_lat = 2 * MLA_HEADS * LANES
    lat = _rms(z[:, o_lat:o_lat + MLA_LORA], glat_ref[...])
    kr = _rope(z[:, o_lat + MLA_LORA:], c, sa, sb, half)
    row_ref[:, :MLA_LORA] = lat
    row_ref[:, MLA_LORA:] = kr[:, :MLA_ROPE]
    k_ref[:, :MLA_LORA] = lat.astype(BF16)
    k_ref[:, MLA_LORA:] = kr.astype(BF16)
    for hh in range(MLA_HEADS):
        qn = z[:, hh * LANES:(hh + 1) * LANES].astype(BF16)
        q_ref[hh, :, :MLA_LORA] = _dot(qn, wuk_ref[hh]).astype(BF16)
        qr = _rope(z[:, o_qr + hh * LANES:o_qr + (hh + 1) * LANES], c, sa, sb, half)
        q_ref[hh, :, MLA_LORA:] = qr.astype(BF16)


def _mla_proj(x, mods, gpre, w1, g_lat, wuk, tables, table_tiles, tm):
    t = x.shape[0]
    tab_spec = pl.BlockSpec((tm, LANES), lambda i: (i % table_tiles, 0))
    const = lambda *shape: pl.BlockSpec(shape, lambda i: (0,) * len(shape), pipeline_mode=pl.Buffered(1))
    return pl.pallas_call(
        _mla_proj_kernel,
        grid=(t // tm,),
        in_specs=[pl.BlockSpec((tm, D_MODEL), lambda i: (i, 0)),
                  _row_spec(D_MODEL), mods.spec(3, tm), mods.spec(4, tm),
                  const(D_MODEL, W1_COLS), _row_spec(MLA_LORA), const(MLA_HEADS, MLA_NOPE, MLA_LORA),
                  tab_spec, tab_spec, tab_spec],
        out_specs=[pl.BlockSpec((MLA_HEADS, tm, MLA_QCAT), lambda i: (0, i, 0)),
                   pl.BlockSpec((tm, MLA_ROW), lambda i: (i, 0)),
                   pl.BlockSpec((tm, MLA_QCAT), lambda i: (i, 0))],
        out_shape=[jax.ShapeDtypeStruct((MLA_HEADS, t, MLA_QCAT), BF16),
                   jax.ShapeDtypeStruct((t, MLA_ROW), F32),
                   jax.ShapeDtypeStruct((t, MLA_QCAT), BF16)],
        compiler_params=_cparams("parallel"),
        name="mla_proj",
    )(x, gpre, mods.arr, mods.arr, w1, g_lat, wuk, *tables)


def _nsa_proj_kernel(x_ref, g_ref, sh_ref, sc_ref, w_ref, wcol_ref, c_ref, sa_ref, sb_ref,
                     q_ref, kv_ref, win_ref, kvb_ref, gate_ref, pe_ref, po_ref):
    h = _modulated(x_ref[...], g_ref[...], sh_ref[...], sc_ref[...]).astype(BF16)
    z = _dot(h, w_ref[...])
    c, sa, sb = c_ref[...], sa_ref[...], sb_ref[...]
    half = NSA_ROT // 2
    tm = z.shape[0]
    for hh in range(NSA_HEADS):
        q_ref[hh] = _rope(z[:, hh * LANES:(hh + 1) * LANES], c, sa, sb, half).astype(BF16)
    n_main = 4 * NSA_KV
    for cb in range(6 * NSA_KV):
        v = z[:, W_QB + cb * LANES:W_QB + (cb + 1) * LANES]
        if (cb // NSA_KV) % 2 == 0:
            v = _rope(v, c, sa, sb, half)
        if cb < n_main:
            kv_ref[:, cb * LANES:(cb + 1) * LANES] = v
        else:
            win_ref[:, (cb - n_main) * LANES:(cb - n_main + 1) * LANES] = v
        if cb >= 2 * NSA_KV:
            kvb_ref[:, (cb - 2 * NSA_KV) * LANES:(cb - 2 * NSA_KV + 1) * LANES] = v.astype(BF16)
    for g in range(NSA_KV):
        zg = z[:, W_QB + W_KVB + g * LANES:W_QB + W_KVB + (g + 1) * LANES]
        gate_ref[g] = jax.nn.sigmoid(zg)
    cmp3 = kv_ref[:, :2 * NSA_KV * LANES].reshape(tm // SEL_BLOCK, SEL_BLOCK, 2 * NSA_KV * LANES)
    wcol = wcol_ref[...][None]
    pe_ref[...] = jnp.sum(cmp3[:, :CMP_BLOCK] * wcol, axis=1)
    po_ref[...] = jnp.sum(cmp3[:, CMP_BLOCK:] * wcol, axis=1)


def _nsa_proj(x, mods, gpre, w2, wcol, tables, table_tiles, tm):
    t = x.shape[0]
    tab_spec = pl.BlockSpec((tm, LANES), lambda i: (i % table_tiles, 0))
    const = lambda *shape: pl.BlockSpec(shape, lambda i: (0,) * len(shape), pipeline_mode=pl.Buffered(1))
    npair = tm // SEL_BLOCK
    cw = 2 * NSA_KV * LANES
    return pl.pallas_call(
        _nsa_proj_kernel,
        grid=(t // tm,),
        in_specs=[pl.BlockSpec((tm, D_MODEL), lambda i: (i, 0)),
                  _row_spec(D_MODEL), mods.spec(3, tm), mods.spec(4, tm),
                  const(D_MODEL, W2_COLS), const(CMP_BLOCK, cw), tab_spec, tab_spec, tab_spec],
        out_specs=[pl.BlockSpec((NSA_HEADS, tm, NSA_HD), lambda i: (0, i, 0)),
                   pl.BlockSpec((tm, 4 * NSA_KV * LANES), lambda i: (i, 0)),
                   pl.BlockSpec((tm, 2 * NSA_KV * LANES), lambda i: (i, 0)),
                   pl.BlockSpec((tm, 4 * NSA_KV * LANES), lambda i: (i, 0)),
                   pl.BlockSpec((NSA_KV, tm, LANES), lambda i: (0, i, 0)),
                   pl.BlockSpec((npair, cw), lambda i: (i, 0)),
                   pl.BlockSpec((npair, cw), lambda i: (i, 0))],
        out_shape=[jax.ShapeDtypeStruct((NSA_HEADS, t, NSA_HD), BF16),
                   jax.ShapeDtypeStruct((t, 4 * NSA_KV * LANES), F32),
                   jax.ShapeDtypeStruct((t, 2 * NSA_KV * LANES), F32),
                   jax.ShapeDtypeStruct((t, 4 * NSA_KV * LANES), BF16),
                   jax.ShapeDtypeStruct((NSA_KV, t, LANES), F32),
                   jax.ShapeDtypeStruct((t // SEL_BLOCK, cw), F32),
                   jax.ShapeDtypeStruct((t // SEL_BLOCK, cw), F32)],
        compiler_params=_cparams("parallel"),
        name="nsa_proj",
    )(x, gpre, mods.arr, mods.arr, w2, wcol, *tables)


def _mla_prompt_kernel(q_ref, k_ref, wuv_ref, o_ref, m_ref, l_ref, acc_ref, *, tk):
    i = pl.program_id(1)
    rows = MLA_HEADS * Q_BLOCK
    q = q_ref[...].reshape(rows, MLA_QCAT)
    m_ref[...] = jnp.full_like(m_ref, NEG_INF)
    l_ref[...] = jnp.zeros_like(l_ref)
    acc_ref[...] = jnp.zeros_like(acc_ref)
    qpos = i * Q_BLOCK + lax.broadcasted_iota(jnp.int32, (Q_BLOCK, tk), 0)
    col = lax.broadcasted_iota(jnp.int32, (Q_BLOCK, tk), 1)
    n_blk = (i * Q_BLOCK + Q_BLOCK + tk - 1) // tk

    def body(j, carry):
        start = pl.multiple_of(j * tk, tk)
        kb = k_ref[pl.ds(start, tk), :]
        s = (_dot_nt(q, kb) * MLA_SCALE).reshape(MLA_HEADS, Q_BLOCK, tk)
        mask = (start + col <= qpos)[None]
        s = jnp.where(mask, s, NEG_INF).reshape(rows, tk)
        m_old = m_ref[...]
        m_new = jnp.maximum(m_old, jnp.max(s, axis=-1, keepdims=True))
        alpha = jnp.exp(m_old - m_new)
        p = jnp.exp(s - m_new)
        l_ref[...] = alpha * l_ref[...] + jnp.sum(p, axis=-1, keepdims=True)
        m_ref[...] = m_new
        acc_ref[...] = alpha * acc_ref[...] + _dot(p.astype(BF16), kb[:, :MLA_LORA])
        return carry

    lax.fori_loop(0, n_blk, body, 0)
    o_lat = acc_ref[...] / l_ref[...]
    for hh in range(MLA_HEADS):
        oh = o_lat[hh * Q_BLOCK:(hh + 1) * Q_BLOCK].astype(BF16)
        o_ref[:, hh * MLA_V:(hh + 1) * MLA_V] = _dot(oh, wuv_ref[hh]).astype(BF16)


def _mla_prompt(q_cat, k_cat, wuv, n, s):
    nqb = s // Q_BLOCK
    tk = min(512, s)
    rows = MLA_HEADS * Q_BLOCK
    return pl.pallas_call(
        functools.partial(_mla_prompt_kernel, tk=tk),
        grid=(n, nqb),
        in_specs=[pl.BlockSpec((MLA_HEADS, Q_BLOCK, MLA_QCAT), lambda b, i: (0, b * nqb + i, 0)),
                  pl.BlockSpec((s, MLA_QCAT), lambda b, i: (b, 0)),
                  pl.BlockSpec((MLA_HEADS, MLA_LORA, MLA_V), lambda b, i: (0, 0, 0))],
        out_specs=pl.BlockSpec((Q_BLOCK, MLA_HEADS * MLA_V), lambda b, i: (b * nqb + i, 0)),
        out_shape=jax.ShapeDtypeStruct((n * s, MLA_HEADS * MLA_V), BF16),
        scratch_shapes=[pltpu.VMEM((rows, 1), F32), pltpu.VMEM((rows, 1), F32),
                        pltpu.VMEM((rows, MLA_LORA), F32)],
        compiler_params=_cparams("parallel", "arbitrary"),
        name="mla_prompt",
    )(q_cat, k_cat, wuv)


def _select_blocks(score, n_sel, k_sel):
    lane = lax.broadcasted_iota(jnp.int32, score.shape, 1)
    rank = jnp.zeros(score.shape, F32)
    for cidx in range(n_sel):
        colv = jnp.broadcast_to(score[:, cidx:cidx + 1], score.shape)
        tie = jnp.where(lane > cidx, 1.0, 0.0)
        rank = rank + jnp.where(colv > score, 1.0, jnp.where(colv == score, tie, 0.0))
    return jnp.where(rank < k_sel, jnp.where(score > 0.5 * NEG_INF, 1.0, 0.0), 0.0)


def _block_scores(imp_sel, qpos, n_sel):
    lane = lax.broadcasted_iota(jnp.int32, imp_sel.shape, 1)
    cur = qpos // SEL_BLOCK
    forced = (lane == 0) | (lane == cur) | (lane == cur - 1)
    score = jnp.where(lane <= cur, jnp.where(forced, FORCE_SCORE, imp_sel), NEG_INF)
    return jnp.where(lane < n_sel, score, -3e38)


def _nsa_prompt_kernel(q_ref, pe_ref, po_ref, pve_ref, pvo_ref, sk_ref, sv_ref, wk_ref, wv_ref, gate_ref, o_ref,
                       m_ref, l_ref, acc_ref, m2_ref, l2_ref, acc2_ref, *, n_cmp, tk):
    i = pl.program_id(2)
    rows = NSA_GROUP * Q_BLOCK
    half = n_cmp // 2
    n_sel = half
    k_sel = min(N_SELECT, n_sel)
    q = q_ref[...].reshape(rows, NSA_HD)

    pad = jnp.zeros((LANES - n_cmp, NSA_HD), F32)
    pk = jnp.concatenate([pe_ref[...], po_ref[...], pad], axis=0).astype(BF16)
    pv = jnp.concatenate([pve_ref[...], pvo_ref[...], pad], axis=0).astype(BF16)
    lane = lax.broadcasted_iota(jnp.int32, (Q_BLOCK, LANES), 1)
    qpos = i * Q_BLOCK + lax.broadcasted_iota(jnp.int32, (Q_BLOCK, LANES), 0)
    bid = jnp.where(lane < half, 2 * lane, 2 * (lane - half) + 1)
    vis = ((lane < n_cmp) & ((bid + 1) * CMP_BLOCK - 1 <= qpos))[None]
    s = (_dot_nt(q, pk) * NSA_SCALE).reshape(NSA_GROUP, Q_BLOCK, LANES)
    s = jnp.where(vis, s, NEG_INF)
    e = jnp.exp(s - jnp.max(s, axis=-1, keepdims=True))
    p = jnp.where(vis, e / jnp.sum(e, axis=-1, keepdims=True), 0.0)
    o_cmp = _dot(p.reshape(rows, LANES).astype(BF16), pv)
    imp = p[0]
    for pp in range(1, NSA_GROUP):
        imp = imp + p[pp]
    imp_sel = imp + pltpu.roll(imp, LANES - half, 1)
    sel = _select_blocks(_block_scores(imp_sel, qpos, n_sel), n_sel, k_sel).astype(BF16)

    def reset(mr, lr, ar):
        mr[...] = jnp.full_like(mr, NEG_INF)
        lr[...] = jnp.zeros_like(lr)
        ar[...] = jnp.zeros_like(ar)

    reset(m_ref, l_ref, acc_ref)
    reset(m2_ref, l2_ref, acc2_ref)
    row_k = lax.broadcasted_iota(jnp.int32, (Q_BLOCK, tk), 0)
    col_k = lax.broadcasted_iota(jnp.int32, (Q_BLOCK, tk), 1)
    blk_r = lax.broadcasted_iota(jnp.int32, (LANES, tk), 0)
    blk_c = lax.broadcasted_iota(jnp.int32, (LANES, tk), 1)
    n_blk = (i * Q_BLOCK + Q_BLOCK + tk - 1) // tk
    all_rows = [slice(0, rows)]

    def sel_body(j, carry):
        start = pl.multiple_of(j * tk, tk)
        expand = jnp.where(blk_r == (start + blk_c) // SEL_BLOCK, 1.0, 0.0).astype(BF16)
        member = _dot(sel, expand)
        mask = jnp.where(member > 0.5, jnp.where(start + col_k <= i * Q_BLOCK + row_k, 1.0, 0.0), 0.0)
        mask = jnp.concatenate([mask] * NSA_GROUP, axis=0) > 0.5
        sc = _dot_nt(q, sk_ref[pl.ds(start, tk), :]) * NSA_SCALE
        _flash_update(m_ref, l_ref, acc_ref, sc, mask, [sv_ref[pl.ds(start, tk), :]], all_rows)
        return carry

    lax.fori_loop(0, n_blk, sel_body, 0)

    row_w = lax.broadcasted_iota(jnp.int32, (Q_BLOCK, Q_BLOCK), 0)
    col_w = lax.broadcasted_iota(jnp.int32, (Q_BLOCK, Q_BLOCK), 1)

    def win_body(jb, carry):
        start = pl.multiple_of(jb * Q_BLOCK, Q_BLOCK)
        d = (i * Q_BLOCK + row_w) - (start + col_w)
        mask = jnp.where(d >= 0, jnp.where(d < WINDOW, 1.0, 0.0), 0.0)
        mask = jnp.concatenate([mask] * NSA_GROUP, axis=0) > 0.5
        sc = _dot_nt(q, wk_ref[pl.ds(start, Q_BLOCK), :]) * NSA_SCALE
        _flash_update(m2_ref, l2_ref, acc2_ref, sc, mask, [wv_ref[pl.ds(start, Q_BLOCK), :]], all_rows)
        return carry

    lax.fori_loop(jnp.maximum(i - WINDOW // Q_BLOCK, 0), i + 1, win_body, 0)

    o_sel = acc_ref[...] / l_ref[...]
    o_win = acc2_ref[...] / l2_ref[...]
    gates = gate_ref[...]
    for pp in range(NSA_GROUP):
        rs = slice(pp * Q_BLOCK, (pp + 1) * Q_BLOCK)
        o = (gates[:, pp:pp + 1] * o_cmp[rs] + gates[:, NSA_GROUP + pp:NSA_GROUP + pp + 1] * o_sel[rs]
             + gates[:, 2 * NSA_GROUP + pp:2 * NSA_GROUP + pp + 1] * o_win[rs])
        o_ref[:, pp * NSA_HD:(pp + 1) * NSA_HD] = o.astype(BF16)


def _nsa_prompt(qb, pool_e, pool_o, kvb, gates, n, s):
    nqb = s // Q_BLOCK
    n_cmp = s // CMP_BLOCK
    half = n_cmp // 2
    tk = min(512, s)
    rows = NSA_GROUP * Q_BLOCK
    pool_spec = lambda off: pl.BlockSpec((half, NSA_HD), lambda b, g, i: (b, off + g))
    kv_spec = lambda off: pl.BlockSpec((s, NSA_HD), lambda b, g, i: (b, off + g))
    stat = lambda w: pltpu.VMEM((rows, w), F32)
    return pl.pallas_call(
        functools.partial(_nsa_prompt_kernel, n_cmp=n_cmp, tk=tk),
        grid=(n, NSA_KV, nqb),
        in_specs=[pl.BlockSpec((NSA_GROUP, Q_BLOCK, NSA_HD), lambda b, g, i: (g, b * nqb + i, 0)),
                  pool_spec(0), pool_spec(0), pool_spec(NSA_KV), pool_spec(NSA_KV),
                  kv_spec(0), kv_spec(NSA_KV), kv_spec(2 * NSA_KV), kv_spec(3 * NSA_KV),
                  pl.BlockSpec((None, Q_BLOCK, LANES), lambda b, g, i: (g, b * nqb + i, 0))],
        out_specs=pl.BlockSpec((Q_BLOCK, NSA_GROUP * NSA_HD), lambda b, g, i: (b * nqb + i, g)),
        out_shape=jax.ShapeDtypeStruct((n * s, NSA_HEADS * NSA_HD), BF16),
        scratch_shapes=[stat(1), stat(1), stat(NSA_HD), stat(1), stat(1), stat(NSA_HD)],
        compiler_params=_cparams("parallel", "parallel", "arbitrary"),
        name="nsa_prompt",
    )(qb, pool_e, pool_o, pool_e, pool_o, kvb, kvb, kvb, kvb, gates)


def _merge_kernel(x_ref, g_ref, sh_ref, sc_ref, gate_ref, gpost_ref, w3_ref, om_ref, on_ref, pa_ref, pb_ref,
                  wo_ref, o_ref):
    x = x_ref[...]
    h = _modulated(x, g_ref[...], sh_ref[...], sc_ref[...]).astype(BF16)
    gm = jax.nn.sigmoid(_dot(h, w3_ref[...]))
    u = gm[:, :D_MODEL] * _dot(om_ref[...], pa_ref[...]) + gm[:, D_MODEL:] * _dot(on_ref[...], pb_ref[...])
    m = _dot(u.astype(BF16), wo_ref[...])
    o_ref[...] = x + gate_ref[...] * _rms(m, gpost_ref[...])


def _merge(x, mods, gpre, gpost, w3, o_mla, o_nsa, pa, pb, wo, tm):
    t = x.shape[0]
    const = lambda *shape: pl.BlockSpec(shape, lambda i: (0,) * len(shape), pipeline_mode=pl.Buffered(1))
    hd = MLA_HEADS * MLA_V
    return pl.pallas_call(
        _merge_kernel,
        grid=(t // tm,),
        in_specs=[pl.BlockSpec((tm, D_MODEL), lambda i: (i, 0)),
                  _row_spec(D_MODEL), mods.spec(3, tm), mods.spec(4, tm), mods.spec(5, tm), _row_spec(D_MODEL),
                  const(D_MODEL, W_MERGE),
                  pl.BlockSpec((tm, hd), lambda i: (i, 0)), pl.BlockSpec((tm, hd), lambda i: (i, 0)),
                  const(hd, D_MODEL), const(hd, D_MODEL), const(D_MODEL, D_MODEL)],
        out_specs=pl.BlockSpec((tm, D_MODEL), lambda i: (i, 0)),
        out_shape=jax.ShapeDtypeStruct((t, D_MODEL), F32),
        compiler_params=_cparams("parallel"),
        name="merge",
    )(x, gpre, mods.arr, mods.arr, mods.arr, gpost, w3, o_mla, o_nsa, pa, pb, wo)


def _page_specs(n_pages, width, col_block):
    def one(k):
        return pl.BlockSpec((None, None, PAGE, width),
                            lambda b, c, pt: (0, pt[b * n_pages + c * PAGES_PER_STEP + k], 0, col_block))
    return [one(k) for k in range(PAGES_PER_STEP)]


def _mla_decode_kernel(pt_ref, q_ref, *refs, t_new):
    pages = refs[:PAGES_PER_STEP]
    knew_ref, wuv_ref, o_ref, m_ref, l_ref, acc_ref = refs[PAGES_PER_STEP:]
    c = pl.program_id(1)
    rows = MLA_HEADS * t_new

    @pl.when(c == 0)
    def _():
        m_ref[...] = jnp.full_like(m_ref, NEG_INF)
        l_ref[...] = jnp.zeros_like(l_ref)
        acc_ref[...] = jnp.zeros_like(acc_ref)

    q = q_ref[...]
    ql, qr = q[:, :MLA_LORA], q[:, MLA_LORA:MLA_ROW]
    kk = jnp.concatenate([p[...] for p in pages], axis=0).astype(BF16)
    s = (_dot_nt(ql, kk[:, :MLA_LORA]) + _dot_nt(qr, kk[:, MLA_LORA:])) * MLA_SCALE
    _flash_update(m_ref, l_ref, acc_ref, s, None, [kk[:, :MLA_LORA]], [slice(0, rows)])

    @pl.when(c == pl.num_programs(1) - 1)
    def _():
        kn = jnp.concatenate([knew_ref[...], jnp.zeros((8 - t_new, MLA_ROW), F32)], axis=0).astype(BF16)
        sn = (_dot_nt(ql, kn[:, :MLA_LORA]) + _dot_nt(qr, kn[:, MLA_LORA:])) * MLA_SCALE
        tq = lax.broadcasted_iota(jnp.int32, (rows, 8), 0) % t_new
        rk = lax.broadcasted_iota(jnp.int32, (rows, 8), 1)
        _flash_update(m_ref, l_ref, acc_ref, sn, rk <= tq, [kn[:, :MLA_LORA]], [slice(0, rows)])
        o_lat = (acc_ref[...] / l_ref[...]).astype(BF16)
        head = lax.broadcasted_iota(jnp.int32, (rows, MLA_V), 0) // t_new
        out = jnp.zeros((rows, MLA_V), F32)
        for hh in range(MLA_HEADS):
            out = out + jnp.where(head == hh, _dot(o_lat, wuv_ref[hh]), 0.0)
        o_ref[...] = out


def _mla_decode(page_table, q_s, cache_mla, knew, wuv, t_new):
    ns, n_pages = page_table.shape
    rows = MLA_HEADS * t_new
    grid_spec = pltpu.PrefetchScalarGridSpec(
        num_scalar_prefetch=1,
        grid=(ns, n_pages // PAGES_PER_STEP),
        in_specs=[pl.BlockSpec((None, rows, MLA_QCAT), lambda b, c, pt: (b, 0, 0))]
        + _page_specs(n_pages, MLA_ROW, 0)
        + [pl.BlockSpec((None, t_new, MLA_ROW), lambda b, c, pt: (b, 0, 0)),
           pl.BlockSpec((MLA_HEADS, MLA_LORA, MLA_V), lambda b, c, pt: (0, 0, 0))],
        out_specs=pl.BlockSpec((None, rows, MLA_V), lambda b, c, pt: (b, 0, 0)),
        scratch_shapes=[pltpu.VMEM((rows, 1), F32), pltpu.VMEM((rows, 1), F32), pltpu.VMEM((rows, MLA_LORA), F32)],
    )
    return pl.pallas_call(
        functools.partial(_mla_decode_kernel, t_new=t_new),
        grid_spec=grid_spec,
        out_shape=jax.ShapeDtypeStruct((ns, rows, MLA_V), F32),
        compiler_params=_cparams("parallel", "arbitrary"),
        name="mla_decode",
    )(page_table.reshape(-1), q_s, *([cache_mla] * PAGES_PER_STEP), knew, wuv)


def _nsa_cmp_kernel(pt_ref, q_ref, *refs, t_new, past):
    pages = refs[:PAGES_PER_STEP]
    wcol_ref, o_ref, sel_ref, pool_ref = refs[PAGES_PER_STEP:]
    c = pl.program_id(1)
    n_cmp = past // CMP_BLOCK
    half = n_cmp // 2
    cw = 2 * NSA_KV * LANES
    pairs = PAGES_PER_STEP * 128 // SEL_BLOCK

    x3 = jnp.concatenate([p[...] for p in pages], axis=0).reshape(pairs, SEL_BLOCK, cw)
    wcol = wcol_ref[...][None]
    start = pl.multiple_of(c * pairs, pairs)
    pool_ref[pl.ds(start, pairs), :] = jnp.sum(x3[:, :CMP_BLOCK] * wcol, axis=1)
    pool_ref[pl.ds(half + start, pairs), :] = jnp.sum(x3[:, CMP_BLOCK:] * wcol, axis=1)

    @pl.when(c == pl.num_programs(1) - 1)
    def _():
        grp_rows = NSA_GROUP * t_new
        n_sel = -(-(past + t_new) // SEL_BLOCK)
        width = -(-n_sel // LANES) * LANES
        lane = lax.broadcasted_iota(jnp.int32, (grp_rows, n_cmp), 1)
        qpos_c = past + lax.broadcasted_iota(jnp.int32, (grp_rows, n_cmp), 0) % t_new
        bid = jnp.where(lane < half, 2 * lane, 2 * (lane - half) + 1)
        vis = (bid + 1) * CMP_BLOCK - 1 <= qpos_c
        imps = []
        for g in range(NSA_KV):
            qg = q_ref[g * grp_rows:(g + 1) * grp_rows, :]
            pk = pool_ref[:, g * LANES:(g + 1) * LANES].astype(BF16)
            pv = pool_ref[:, (NSA_KV + g) * LANES:(NSA_KV + g + 1) * LANES].astype(BF16)
            s = jnp.where(vis, _dot_nt(qg, pk) * NSA_SCALE, NEG_INF)
            e = jnp.exp(s - jnp.max(s, axis=-1, keepdims=True))
            p = jnp.where(vis, e / jnp.sum(e, axis=-1, keepdims=True), 0.0)
            o_ref[g * grp_rows:(g + 1) * grp_rows, :] = _dot(p.astype(BF16), pv)
            tot = p[0:8]
            for r in range(8, grp_rows, 8):
                tot = tot + p[r:r + 8]
            sh = t_new
            while sh < 8:
                tot = tot + pltpu.roll(tot, sh, 0)
                sh *= 2
            imps.append(tot)
        sub = lax.broadcasted_iota(jnp.int32, (8, n_cmp), 0)
        imp = imps[0]
        for g in range(1, NSA_KV):
            imp = jnp.where(sub // t_new == g, imps[g], imp)
        imp_sel = jnp.concatenate([imp[:, :half] + imp[:, half:], jnp.zeros((8, width - half), F32)], axis=1)
        qpos = past + lax.broadcasted_iota(jnp.int32, (8, width), 0) % t_new
        sel_ref[...] = _select_blocks(_block_scores(imp_sel, qpos, n_sel), n_sel, min(N_SELECT, n_sel))


def _nsa_cmp_decode(page_table, q_s, cache_nsa, wcol, t_new):
    ns, n_pages = page_table.shape
    past = n_pages * 128
    rows = NSA_HEADS * t_new
    n_sel = -(-(past + t_new) // SEL_BLOCK)
    width = -(-n_sel // LANES) * LANES
    cw = 2 * NSA_KV * LANES
    assert NSA_KV * t_new == 8 and (past // CMP_BLOCK // 2) % LANES == 0
    grid_spec = pltpu.PrefetchScalarGridSpec(
        num_scalar_prefetch=1,
        grid=(ns, n_pages // PAGES_PER_STEP),
        in_specs=[pl.BlockSpec((None, rows, NSA_HD), lambda b, c, pt: (b, 0, 0))]
        + _page_specs(n_pages, cw, 0)
        + [pl.BlockSpec((CMP_BLOCK, cw), lambda b, c, pt: (0, 0))],
        out_specs=[pl.BlockSpec((None, rows, NSA_HD), lambda b, c, pt: (b, 0, 0)),
                   pl.BlockSpec((None, 8, width), lambda b, c, pt: (b, 0, 0))],
        scratch_shapes=[pltpu.VMEM((past // CMP_BLOCK, cw), F32)],
    )
    return pl.pallas_call(
        functools.partial(_nsa_cmp_kernel, t_new=t_new, past=past),
        grid_spec=grid_spec,
        out_shape=[jax.ShapeDtypeStruct((ns, rows, NSA_HD), F32),
                   jax.ShapeDtypeStruct((ns, 8, width), F32)],
        compiler_params=_cparams("parallel", "arbitrary"),
        name="nsa_cmp_decode",
    )(page_table.reshape(-1), q_s, *([cache_nsa] * PAGES_PER_STEP), wcol)


def _nsa_sel_kernel(pt_ref, q_ref, sel_ref, *refs, t_new, past):
    pages = refs[:PAGES_PER_STEP]
    knew_ref, o_ref, m_ref, l_ref, acc_ref = refs[PAGES_PER_STEP:]
    c = pl.program_id(1)
    rows = NSA_HEADS * t_new
    grp_rows = NSA_GROUP * t_new
    keys = PAGES_PER_STEP * 128
    width = sel_ref.shape[-1]
    row_slices = [slice(g * grp_rows, (g + 1) * grp_rows) for g in range(NSA_KV)]

    @pl.when(c == 0)
    def _():
        m_ref[...] = jnp.full_like(m_ref, NEG_INF)
        l_ref[...] = jnp.zeros_like(l_ref)
        acc_ref[...] = jnp.zeros_like(acc_ref)

    q = q_ref[...]
    sel = sel_ref[...].astype(BF16)
    x = jnp.concatenate([p[...] for p in pages], axis=0).astype(BF16)
    blk_r = lax.broadcasted_iota(jnp.int32, (width, keys), 0)
    blk_c = lax.broadcasted_iota(jnp.int32, (width, keys), 1)
    expand = jnp.where(blk_r == (c * keys + blk_c) // SEL_BLOCK, 1.0, 0.0).astype(BF16)
    mask = _dot(sel, expand) > 0.5
    s = jnp.concatenate([_dot_nt(q[rs], x[:, g * LANES:(g + 1) * LANES]) for g, rs in enumerate(row_slices)],
                        axis=0) * NSA_SCALE
    vs = [x[:, (NSA_KV + g) * LANES:(NSA_KV + g + 1) * LANES] for g in range(NSA_KV)]
    _flash_update(m_ref, l_ref, acc_ref, s, mask, vs, row_slices)

    @pl.when(c == pl.num_programs(1) - 1)
    def _():
        kn = jnp.concatenate([knew_ref[...], jnp.zeros((8 - t_new, 2 * NSA_KV * LANES), F32)], axis=0).astype(BF16)
        sn = jnp.concatenate([_dot_nt(q[rs], kn[:, g * LANES:(g + 1) * LANES]) for g, rs in enumerate(row_slices)],
                             axis=0) * NSA_SCALE
        nr = lax.broadcasted_iota(jnp.int32, (width, 8), 0)
        member = _dot(sel, jnp.where(nr == past // SEL_BLOCK, 1.0, 0.0).astype(BF16)) > 0.5
        tq = lax.broadcasted_iota(jnp.int32, (rows, 8), 0) % t_new
        rk = lax.broadcasted_iota(jnp.int32, (rows, 8), 1)
        maskn = member & (rk <= tq)
        vn = [kn[:, (NSA_KV + g) * LANES:(NSA_KV + g + 1) * LANES] for g in range(NSA_KV)]
        _flash_update(m_ref, l_ref, acc_ref, sn, maskn, vn, row_slices)
        o_ref[...] = acc_ref[...] / l_ref[...]


def _nsa_sel_decode(page_table, q_s, sel_rows, cache_nsa, knew, t_new):
    ns, n_pages = page_table.shape
    past = n_pages * 128
    rows = NSA_HEADS * t_new
    cw = 2 * NSA_KV * LANES
    width = sel_rows.shape[-1]
    assert past % SEL_BLOCK == 0 and t_new <= 8
    grid_spec = pltpu.PrefetchScalarGridSpec(
        num_scalar_prefetch=1,
        grid=(ns, n_pages // PAGES_PER_STEP),
        in_specs=[pl.BlockSpec((None, rows, NSA_HD), lambda b, c, pt: (b, 0, 0)),
                  pl.BlockSpec((None, rows, width), lambda b, c, pt: (b, 0, 0))]
        + _page_specs(n_pages, cw, 1)
        + [pl.BlockSpec((None, t_new, cw), lambda b, c, pt: (b, 0, 1))],
        out_specs=pl.BlockSpec((None, rows, NSA_HD), lambda b, c, pt: (b, 0, 0)),
        scratch_shapes=[pltpu.VMEM((rows, 1), F32), pltpu.VMEM((rows, 1), F32), pltpu.VMEM((rows, NSA_HD), F32)],
    )
    return pl.pallas_call(
        functools.partial(_nsa_sel_kernel, t_new=t_new, past=past),
        grid_spec=grid_spec,
        out_shape=jax.ShapeDtypeStruct((ns, rows, NSA_HD), F32),
        compiler_params=_cparams("parallel", "arbitrary"),
        name="nsa_sel_decode",
    )(page_table.reshape(-1), q_s, sel_rows, *([cache_nsa] * PAGES_PER_STEP), knew)


def _nsa_win_kernel(q_ref, st_ref, knew_ref, ocmp_ref, osel_ref, gate_ref, o_ref, win_ref, *, t_new):
    rows = NSA_HEADS * t_new
    grp_rows = NSA_GROUP * t_new
    wb = st_ref.shape[0]
    st = st_ref[...].astype(BF16)
    kn = jnp.concatenate([knew_ref[...], jnp.zeros((8 - t_new, 2 * NSA_KV * LANES), F32)], axis=0).astype(BF16)
    tq = lax.broadcasted_iota(jnp.int32, (grp_rows, wb), 0) % t_new
    d = wb + tq - lax.broadcasted_iota(jnp.int32, (grp_rows, wb), 1)
    mask_s = (d >= 0) & (d < WINDOW)
    tq8 = lax.broadcasted_iota(jnp.int32, (grp_rows, 8), 0) % t_new
    r8 = lax.broadcasted_iota(jnp.int32, (grp_rows, 8), 1)
    mask_n = (r8 < t_new) & (r8 <= tq8)
    gates = gate_ref[...]
    for g in range(NSA_KV):
        rs = slice(g * grp_rows, (g + 1) * grp_rows)
        qg = q_ref[rs, :]
        ss = jnp.where(mask_s, _dot_nt(qg, st[:, g * LANES:(g + 1) * LANES]) * NSA_SCALE, NEG_INF)
        sn = jnp.where(mask_n, _dot_nt(qg, kn[:, g * LANES:(g + 1) * LANES]) * NSA_SCALE, NEG_INF)
        m = jnp.maximum(jnp.max(ss, axis=-1, keepdims=True), jnp.max(sn, axis=-1, keepdims=True))
        ps = jnp.where(mask_s, jnp.exp(ss - m), 0.0)
        pn = jnp.where(mask_n, jnp.exp(sn - m), 0.0)
        den = jnp.sum(ps, axis=-1, keepdims=True) + jnp.sum(pn, axis=-1, keepdims=True)
        o_win = (_dot(ps.astype(BF16), st[:, (NSA_KV + g) * LANES:(NSA_KV + g + 1) * LANES])
                 + _dot(pn.astype(BF16), kn[:, (NSA_KV + g) * LANES:(NSA_KV + g + 1) * LANES])) / den
        gt = gates[rs]
        o_ref[rs, :] = gt[:, 0:1] * ocmp_ref[rs, :] + gt[:, 1:2] * osel_ref[rs, :] + gt[:, 2:3] * o_win
    win_ref[0:wb - t_new, :] = st_ref[t_new:wb, :]
    win_ref[wb - t_new:wb, :] = knew_ref[...]


def _nsa_win_decode(q_s, state, knew, o_cmp, o_sel, gates, t_new):
    ns, wb, cw = state.shape
    rows = NSA_HEADS * t_new
    per_seq = lambda r, w: pl.BlockSpec((None, r, w), lambda b: (b, 0, 0))
    return pl.pallas_call(
        functools.partial(_nsa_win_kernel, t_new=t_new),
        grid=(ns,),
        in_specs=[per_seq(rows, NSA_HD), per_seq(wb, cw), per_seq(t_new, cw), per_seq(rows, NSA_HD),
                  per_seq(rows, NSA_HD), per_seq(rows, LANES)],
        out_specs=[per_seq(rows, NSA_HD), per_seq(wb, cw)],
        out_shape=[jax.ShapeDtypeStruct((ns, rows, NSA_HD), F32), jax.ShapeDtypeStruct((ns, wb, cw), F32)],
        compiler_params=_cparams("parallel"),
        name="nsa_win_decode",
    )(q_s, state, knew, o_cmp, o_sel, gates)


def _pack_w_in(w_in):
    d = w_in.shape[0]
    o = 0
    qa = w_in[:, o:o + W_QA].reshape(d, MLA_HEADS, MLA_NOPE + MLA_ROPE)
    o += W_QA
    kva = w_in[:, o:o + W_KVA]
    o += W_KVA
    w_qb = w_in[:, o:o + W_QB]
    o += W_QB
    w_kvb = w_in[:, o:o + W_KVB]
    o += W_KVB
    w_g = w_in[:, o:o + W_GB].reshape(d, 3, NSA_KV, NSA_GROUP)
    o += W_GB
    w3 = w_in[:, o:o + W_MERGE]
    pad = lambda a, w: jnp.pad(a, [(0, 0)] * (a.ndim - 1) + [(0, w - a.shape[-1])])
    w_qn = qa[:, :, :MLA_NOPE].reshape(d, MLA_HEADS * MLA_NOPE)
    w_qr = pad(qa[:, :, MLA_NOPE:], LANES).reshape(d, MLA_HEADS * LANES)
    w1 = jnp.concatenate([w_qn, w_qr, kva[:, :MLA_LORA], pad(kva[:, MLA_LORA:], LANES)], axis=1)
    gates = [pad(w_g[:, :, g, :].reshape(d, 3 * NSA_GROUP), LANES) for g in range(NSA_KV)]
    w2 = jnp.concatenate([w_qb, w_kvb] + gates, axis=1)
    return w1.astype(BF16), w2.astype(BF16), w3.astype(BF16)


def _heads_to_seq_rows(a, ns, t_new):
    h, _, w = a.shape
    return a.reshape(h, ns, t_new, w).transpose(1, 0, 2, 3).reshape(ns, h * t_new, w)


def _seq_rows_to_tokens(a, ns, t_new):
    w = a.shape[-1]
    h = a.shape[1] // t_new
    return a.reshape(ns, h, t_new, w).transpose(0, 2, 1, 3).reshape(ns * t_new, h * w)


def kernel(x_prompt, x_sample, c_prompt, c_sample, cache_mla, cache_nsa_kv, state_win_kv, page_table, w_ada, b_ada, g_pre_ffn1, g_post_ffn1, w1_gate, w1_up, w1_down, g_pre_mix, g_post_mix, w_in, g_lat, w_uk, w_uv, w_pool, p_a, p_b, w_out, g_pre_ffn2, g_post_ffn2, w2_gate, w2_up, w2_down):
    n, s, d = x_prompt.shape
    ns, t_new, _ = x_sample.shape
    n_pages = page_table.shape[1]
    page = cache_mla.shape[2]
    past = n_pages * page
    depth = w_ada.shape[0]
    assert depth == 1 and page == 128 and d == D_MODEL
    tp, ts = n * s, ns * t_new
    l = 0

    tm_p = min(512, s)
    tm_s = min(512, ts)
    tm_proj = min(512, s)
    tm_merge = min(256, s)

    c_all = jnp.concatenate([jnp.repeat(c_sample, t_new, axis=0), c_prompt], axis=0)
    c_all = jnp.pad(c_all, ((0, -c_all.shape[0] % 8), (0, 0)))
    ada = _adaln(c_all, w_ada[l], b_ada[l])
    mods_s = _Mods(ada, True)
    ada_p = ada[ts:ts + n].reshape(n, N_ADA, 1, D_MODEL)

    def mods_p(tm):
        return _Mods(ada_p, False, s // tm)

    row = lambda a: a[l].reshape(1, -1)
    bf = lambda a: a[l].astype(BF16)
    xp = x_prompt.reshape(tp, d)
    xs = x_sample.reshape(ts, d)

    ffn1 = (row(g_pre_ffn1), row(g_post_ffn1), bf(w1_gate), bf(w1_up), bf(w1_down))
    xp = _ffn(xp, mods_p(tm_p), 0, *ffn1, tm_p)
    xs = _ffn(xs, mods_s, 0, *ffn1, tm_s)

    w1, w2, w3 = _pack_w_in(w_in[l])
    wuk = jnp.transpose(w_uk[l], (1, 2, 0)).astype(BF16)
    wuv = jnp.transpose(w_uv[l], (1, 0, 2)).astype(BF16)
    cw = 2 * NSA_KV * LANES
    wcol = jnp.concatenate([jnp.broadcast_to(w_pool[l, 0][:, None], (CMP_BLOCK, cw // 2)),
                            jnp.broadcast_to(w_pool[l, 1][:, None], (CMP_BLOCK, cw // 2))], axis=1)
    pos_p = jnp.arange(s)
    pos_s = jnp.tile(past + jnp.arange(t_new), ns)
    gmix = row(g_pre_mix)
    glat = row(g_lat)

    q_cat_p, mla_row_p, k_cat_p = _mla_proj(xp, mods_p(tm_proj), gmix, w1, glat, wuk,
                                            _rope_tables(pos_p, MLA_ROPE), s // tm_proj, tm_proj)
    q_cat_s, mla_row_s, _ = _mla_proj(xs, mods_s, gmix, w1, glat, wuk,
                                      _rope_tables(pos_s, MLA_ROPE), 1, tm_s)
    qb_p, kv_p, win_p, kvb_p, gates_p, pool_e, pool_o = _nsa_proj(
        xp, mods_p(tm_proj), gmix, w2, wcol, _rope_tables(pos_p, NSA_ROT), s // tm_proj, tm_proj)
    qb_s, kv_s, win_s, _, gates_s, _, _ = _nsa_proj(
        xs, mods_s, gmix, w2, wcol, _rope_tables(pos_s, NSA_ROT), 1, tm_s)

    o_mla_p = _mla_prompt(q_cat_p, k_cat_p, wuv, n, s)
    o_nsa_p = _nsa_prompt(qb_p, pool_e, pool_o, kvb_p, gates_p, n, s)

    cache_nsa = cache_nsa_kv.reshape(depth, cache_nsa_kv.shape[1], page, 4 * NSA_KV * NSA_HD)
    q_mla_s = _heads_to_seq_rows(q_cat_s, ns, t_new)
    q_nsa_s = _heads_to_seq_rows(qb_s, ns, t_new)
    o_mla_s = _mla_decode(page_table, q_mla_s, cache_mla, mla_row_s.reshape(ns, t_new, MLA_ROW), wuv, t_new)
    o_cmp_s, sel8 = _nsa_cmp_decode(page_table, q_nsa_s, cache_nsa, wcol, t_new)
    width = sel8.shape[-1]
    sel_rows = jnp.broadcast_to(sel8.reshape(ns, NSA_KV, 1, t_new, width),
                                (ns, NSA_KV, NSA_GROUP, t_new, width)).reshape(ns, NSA_HEADS * t_new, width)
    kv_s3 = kv_s.reshape(ns, t_new, 4 * NSA_KV * LANES)
    o_sel_s = _nsa_sel_decode(page_table, q_nsa_s, sel_rows, cache_nsa, kv_s3, t_new)
    gates_rows = gates_s[:, :, :3 * NSA_GROUP].reshape(NSA_KV, ns, t_new, 3, NSA_GROUP)
    gates_rows = gates_rows.transpose(1, 0, 4, 2, 3).reshape(ns, NSA_HEADS * t_new, 3)
    gates_rows = jnp.pad(gates_rows, ((0, 0), (0, 0), (0, LANES - 3)))
    wb = state_win_kv.shape[2]
    state = state_win_kv[l].reshape(ns, wb, cw)
    o_nsa_s, new_win_s = _nsa_win_decode(q_nsa_s, state, win_s.reshape(ns, t_new, cw), o_cmp_s, o_sel_s,
                                         gates_rows, t_new)
    o_mla_s = _seq_rows_to_tokens(o_mla_s, ns, t_new).astype(BF16)
    o_nsa_s = _seq_rows_to_tokens(o_nsa_s, ns, t_new).astype(BF16)

    mrg = (gmix, row(g_post_mix), w3)
    tail = (bf(p_a), bf(p_b), bf(w_out))
    xp = _merge(xp, mods_p(tm_merge), *mrg, o_mla_p, o_nsa_p, *tail, tm_merge)
    xs = _merge(xs, mods_s, *mrg, o_mla_s, o_nsa_s, *tail, min(128, ts))

    ffn2 = (row(g_pre_ffn2), row(g_post_ffn2), bf(w2_gate), bf(w2_up), bf(w2_down))
    xp = _ffn(xp, mods_p(tm_p), 6, *ffn2, tm_p)
    xs = _ffn(xs, mods_s, 6, *ffn2, tm_s)

    n_keep = min(WINDOW, s)
    return (xp.reshape(n, s, d),
            xs.reshape(ns, t_new, d),
            mla_row_p.reshape(1, n, s, MLA_ROW),
            kv_p.reshape(1, n, s, 4, NSA_KV, NSA_HD),
            win_p.reshape(n, s, 2, NSA_KV, NSA_HD)[:, s - n_keep:][None],
            mla_row_s.reshape(1, ns, t_new, MLA_ROW),
            kv_s.reshape(1, ns, t_new, 4, NSA_KV, NSA_HD),
            new_win_s.reshape(1, ns, wb, 2, NSA_KV, NSA_HD))
```
